```python
import jax, jax.numpy as jnp
from jax import lax
import numpy as np

D_MODEL = 2048
BATCH = 8
SEQ = 2048
DEPTH = 1

CHUNK = 64
EPS = 1e-6
NEG_INF = -1e30

MIX_WIDTH = D_MODEL
POOL_WIDTH = D_MODEL // 2
POOL_WINDOWS = (2, 4, 8, 16)
POOL_GROUPS = len(POOL_WINDOWS)
POOL_GROUP_DIM = POOL_WIDTH // POOL_GROUPS
V_HEAD_DIM = 128
MLA_WIDTH = MIX_WIDTH - POOL_WIDTH
MLA_HEADS = MLA_WIDTH // V_HEAD_DIM
QK_NOPE_DIM = 128
QK_ROPE_DIM = 64
QK_HEAD_DIM = QK_NOPE_DIM + QK_ROPE_DIM
Q_LORA_RANK = 512
KV_LORA_RANK = 256
ROPE_THETA = 10000.0
IN_WIDTH = POOL_WIDTH + Q_LORA_RANK + KV_LORA_RANK + QK_ROPE_DIM
Q_BLOCK = 128

PEER_HEADS = 8
PEER_N_KEYS = 128
PEER_N_EXPERTS = PEER_N_KEYS * PEER_N_KEYS
PEER_QUERY_DIM = 256
PEER_HALF_DIM = PEER_QUERY_DIM // 2
PEER_TOPK = 16
PEER_TOKEN_BLOCK = 128

kernel_name = "hybrid_pool_mla_peer_block"


def rms_norm(x, g):
    xf = x.astype(jnp.float32)
    y = xf * lax.rsqrt(jnp.mean(xf * xf, axis=-1, keepdims=True) + EPS)
    return (y * g.astype(jnp.float32)).astype(x.dtype)


def modulate(h, shift, scale):
    return h * (1 + scale[:, None, :]) + shift[:, None, :]


def apply_rope(x, cos, sin):
    x1, x2 = jnp.split(x.astype(jnp.float32), 2, axis=-1)
    return jnp.concatenate([x1 * cos - x2 * sin, x2 * cos + x1 * sin], axis=-1).astype(x.dtype)


def multiscale_pool(u, w_pool, pool_scale):
    b, s, _ = u.shape
    ug = u.astype(jnp.float32).reshape(b, s, POOL_GROUPS, POOL_GROUP_DIM)
    cs0 = jnp.concatenate([jnp.zeros_like(ug[:, :1]), jnp.cumsum(ug, axis=1)], axis=1)
    t = jnp.arange(s)
    groups = []
    for g, w in enumerate(POOL_WINDOWS):
        c_g = cs0[:, :, g]
        lagged = jnp.pad(c_g[:, : s + 1 - w], ((0, 0), (w - 1, 0), (0, 0)))
        count = jnp.minimum(t + 1, w).astype(jnp.float32)[None, :, None]
        groups.append((c_g[:, 1:] - lagged) / count - ug[:, :, g])
    z = jnp.stack(groups, axis=2).astype(u.dtype)
    y = jnp.einsum('bsgc,gcd->bsgd', z, w_pool)
    return y.reshape(b, s, POOL_WIDTH) * pool_scale


def mla_attention(c_q, c_kv, k_pe, cos, sin, q_norm_g, w_uq, kv_norm_g, w_ukv):
    b, s, _ = c_q.shape
    q = (rms_norm(c_q, q_norm_g) @ w_uq).reshape(b, s, MLA_HEADS, QK_HEAD_DIM)
    q_nope = q[..., :QK_NOPE_DIM]
    q_pe = apply_rope(q[..., QK_NOPE_DIM:], cos[:, :, None, :], sin[:, :, None, :])
    kv = (rms_norm(c_kv, kv_norm_g) @ w_ukv).reshape(b, s, MLA_HEADS, QK_NOPE_DIM + V_HEAD_DIM)
    k_nope = kv[..., :QK_NOPE_DIM]
    v = kv[..., QK_NOPE_DIM:]
    k_pe = apply_rope(k_pe, cos, sin)
    scale = QK_HEAD_DIM ** -0.5
    chunk_id = jnp.arange(s) // CHUNK
    outs = []
    for i in range(s // Q_BLOCK):
        qs, qe = i * Q_BLOCK, (i + 1) * Q_BLOCK
        kend = qe
        sc = (jnp.einsum('bqhd,bkhd->bhqk', q_nope[:, qs:qe], k_nope[:, :kend])
              + jnp.einsum('bqhr,bkr->bhqk', q_pe[:, qs:qe], k_pe[:, :kend])).astype(jnp.float32) * scale
        mask = chunk_id[qs:qe, None] >= chunk_id[None, :kend]
        sc = jnp.where(mask[None, None], sc, NEG_INF)
        p = jax.nn.softmax(sc, axis=-1).astype(v.dtype)
        outs.append(jnp.einsum('bhqk,bkhd->bqhd', p, v[:, :kend]))
    return jnp.concatenate(outs, axis=1).reshape(b, s, MLA_WIDTH)


def peer_ffn(h, w_q, sub_keys, expert_u, expert_v):
    b, s, d = h.shape
    t = b * s
    hf = h.reshape(t, d)
    q = (hf @ w_q).reshape(t, PEER_HEADS, 2, PEER_HALF_DIM)
    sc = jnp.einsum('thpd,hpnd->thpn', q, sub_keys).astype(jnp.float32)
    sv, si = lax.top_k(sc, PEER_TOPK)
    cand = (sv[:, :, 0, :, None] + sv[:, :, 1, None, :]).reshape(t, PEER_HEADS, PEER_TOPK * PEER_TOPK)
    cidx = (si[:, :, 0, :, None] * PEER_N_KEYS + si[:, :, 1, None, :]).reshape(t, PEER_HEADS, PEER_TOPK * PEER_TOPK)
    best, pos = lax.top_k(cand, PEER_TOPK)
    eidx = jnp.take_along_axis(cidx, pos, axis=-1).reshape(t, PEER_HEADS * PEER_TOPK)
    gate = jax.nn.softmax(best, axis=-1).reshape(t, PEER_HEADS * PEER_TOPK)
    nb = t // PEER_TOKEN_BLOCK

    def block(args):
        xb, ib, gb = args
        u = jnp.take(expert_u, ib, axis=0)
        a = jnp.einsum('ted,td->te', u, xb).astype(jnp.float32)
        act = (jax.nn.gelu(a, approximate=False) * gb).astype(xb.dtype)
        vv = jnp.take(expert_v, ib, axis=0)
        return jnp.einsum('te,ted->td', act, vv)

    out = lax.map(block, (hf.reshape(nb, PEER_TOKEN_BLOCK, d),
                          eidx.reshape(nb, PEER_TOKEN_BLOCK, -1),
                          gate.reshape(nb, PEER_TOKEN_BLOCK, -1)))
    return out.reshape(b, s, d)


def setup_inputs(seed: int = 0) -> dict:
    key = jax.random.key(seed)
    ks = jax.random.split(key, 24)
    f32 = jnp.float32

    def nrm(k, shape, s):
        return jax.random.normal(k, shape, f32) * s

    def gain(k, shape):
        return 1.0 + 0.05 * jax.random.normal(k, shape, f32)

    x = jax.random.normal(ks[0], (BATCH, SEQ, D_MODEL), f32)
    c = jax.random.normal(ks[1], (BATCH, D_MODEL), f32)
    offset = jax.random.randint(ks[2], (BATCH, 1), 0, 64, dtype=jnp.int32) * CHUNK
    positions = (offset + jnp.arange(SEQ, dtype=jnp.int32)[None, :]).astype(jnp.int32)
    L = DEPTH
    return {
        "x": x,
        "c": c,
        "positions": positions,
        "ada_w": nrm(ks[3], (L, D_MODEL, 6 * D_MODEL), 0.5 * D_MODEL ** -0.5),
        "ada_b": nrm(ks[4], (L, 6 * D_MODEL), 0.02),
        "pre_mix_g": gain(ks[5], (L, D_MODEL)),
        "post_mix_g": gain(ks[6], (L, D_MODEL)),
        "pre_ffn_g": gain(ks[7], (L, D_MODEL)),
        "post_ffn_g": gain(ks[8], (L, D_MODEL)),
        "w_in": nrm(ks[9], (L, D_MODEL, IN_WIDTH), D_MODEL ** -0.5),
        "pool_w": nrm(ks[10], (L, POOL_GROUPS, POOL_GROUP_DIM, POOL_GROUP_DIM), POOL_GROUP_DIM ** -0.5),
        "pool_scale": 1.0 + 0.1 * jax.random.normal(ks[11], (L, POOL_WIDTH), f32),
        "q_norm_g": gain(ks[12], (L, Q_LORA_RANK)),
        "w_uq": nrm(ks[13], (L, Q_LORA_RANK, MLA_HEADS * QK_HEAD_DIM), Q_LORA_RANK ** -0.5),
        "kv_norm_g": gain(ks[14], (L, KV_LORA_RANK)),
        "w_ukv": nrm(ks[15], (L, KV_LORA_RANK, MLA_HEADS * (QK_NOPE_DIM + V_HEAD_DIM)), KV_LORA_RANK ** -0.5),
        "w_out": nrm(ks[16], (L, MIX_WIDTH, D_MODEL), MIX_WIDTH ** -0.5),
        "peer_wq": nrm(ks[17], (L, D_MODEL, PEER_HEADS * PEER_QUERY_DIM), D_MODEL ** -0.5),
        "peer_sub_keys": nrm(ks[18], (L, PEER_HEADS, 2, PEER_N_KEYS, PEER_HALF_DIM), PEER_HALF_DIM ** -0.5),
        "peer_u": nrm(ks[19], (L, PEER_N_EXPERTS, D_MODEL), D_MODEL ** -0.5),
        "peer_v": nrm(ks[20], (L, PEER_N_EXPERTS, D_MODEL), PEER_HEADS ** -0.5),
    }


def reference(x, c, positions, ada_w, ada_b, pre_mix_g, post_mix_g, pre_ffn_g, post_ffn_g,
              w_in, pool_w, pool_scale, q_norm_g, w_uq, kv_norm_g, w_ukv, w_out,
              peer_wq, peer_sub_keys, peer_u, peer_v):
    half = QK_ROPE_DIM // 2
    inv_freq = ROPE_THETA ** (-jnp.arange(half, dtype=jnp.float32) / half)
    ang = positions.astype(jnp.float32)[..., None] * inv_freq
    cos, sin = jnp.cos(ang), jnp.sin(ang)
    c_act = jax.nn.silu(c)
    for l in range(DEPTH):
        mod = c_act @ ada_w[l] + ada_b[l]
        shift1, scale1, gate1, shift2, scale2, gate2 = jnp.split(mod, 6, axis=-1)

        h = modulate(rms_norm(x, pre_mix_g[l]), shift1, scale1)
        proj = h @ w_in[l]
        o1 = POOL_WIDTH
        o2 = o1 + Q_LORA_RANK
        o3 = o2 + KV_LORA_RANK
        y_pool = multiscale_pool(proj[..., :o1], pool_w[l], pool_scale[l])
        y_mla = mla_attention(proj[..., o1:o2], proj[..., o2:o3], proj[..., o3:], cos, sin,
                              q_norm_g[l], w_uq[l], kv_norm_g[l], w_ukv[l])
        y = jnp.concatenate([y_pool, y_mla], axis=-1) @ w_out[l]
        x = x + gate1[:, None, :] * rms_norm(y, post_mix_g[l])

        h = modulate(rms_norm(x, pre_ffn_g[l]), shift2, scale2)
        y = peer_ffn(h, peer_wq[l], peer_sub_keys[l], peer_u[l], peer_v[l])
        x = x + gate2[:, None, :] * rms_norm(y, post_ffn_g[l])
    return x
```

```python
import functools
import math

import jax
import jax.numpy as jnp
from jax import lax
from jax.experimental import pallas as pl
from jax.experimental.pallas import tpu as pltpu

F32 = jnp.float32
BF16 = jnp.bfloat16

EPS = 1e-6
NEG_INF = -1e30
CHUNK = 64

POOL_WINDOWS = (2, 4, 8, 16)
POOL_GROUP_DIM = 256
POOL_WIDTH = 1024
POOL_HALO = 16

MLA_HEADS = 8
QK_NOPE_DIM = 128
QK_ROPE_DIM = 64
QK_HEAD_DIM = QK_NOPE_DIM + QK_ROPE_DIM
QK_PAD_DIM = 256
V_HEAD_DIM = 128
Q_LORA_RANK = 512
KV_LORA_RANK = 256
ROPE_THETA = 10000.0

PEER_HEADS = 8
PEER_N_KEYS = 128
PEER_HALF_DIM = 128
PEER_TOPK = 16

LANES = 128
VMEM_LIMIT_BYTES = 56 * 1024 * 1024

_NT = (((1,), (1,)), ((), ()))


def _rms(x):
    return x * lax.rsqrt(jnp.mean(x * x, axis=-1, keepdims=True) + EPS)


def _params(*semantics):
    return pltpu.CompilerParams(dimension_semantics=semantics, vmem_limit_bytes=VMEM_LIMIT_BYTES)


def _ada_kernel(c_ref, w_ref, b_ref, o_ref):
    c = c_ref[...]
    c_act = (c * jax.nn.sigmoid(c)).astype(BF16)
    o_ref[...] = jnp.dot(c_act, w_ref[...].astype(BF16), preferred_element_type=F32) + b_ref[...]


def _ada(c, w, b, block_n=1024):
    bsz, d = c.shape
    n = w.shape[1]
    return pl.pallas_call(
        _ada_kernel,
        grid=(n // block_n,),
        in_specs=[pl.BlockSpec((bsz, d), lambda j: (0, 0)),
                  pl.BlockSpec((d, block_n), lambda j: (0, j)),
                  pl.BlockSpec((1, block_n), lambda j: (0, j))],
        out_specs=pl.BlockSpec((bsz, block_n), lambda j: (0, j)),
        out_shape=jax.ShapeDtypeStruct((bsz, n), F32),
        compiler_params=_params("parallel"),
        name="ada",
    )(c, w, b.reshape(1, n))


def _mix_in_kernel(x_ref, pos_ref, shift_ref, scale_ref, g_ref, win_ref, pw_ref, ps_ref,
                   qg_ref, wq_ref, kvg_ref, wkv_ref, invf_ref,
                   yp_ref, q_ref, k_ref, v_ref, ext_ref, *, tb):
    i = pl.program_id(1)
    h = _rms(x_ref[...]) * g_ref[...]
    h = h * (1.0 + scale_ref[...]) + shift_ref[...]
    proj = jnp.dot(h.astype(BF16), win_ref[...], preferred_element_type=F32)

    u = proj[:, :POOL_WIDTH]

    @pl.when(i == 0)
    def _():
        ext_ref[0:POOL_HALO, :] = jnp.zeros((POOL_HALO, POOL_WIDTH), F32)

    ext_ref[POOL_HALO:, :] = u
    t = i * tb + lax.broadcasted_iota(jnp.int32, (tb, POOL_GROUP_DIM), 0)
    for g, w in enumerate(POOL_WINDOWS):
        cols = slice(g * POOL_GROUP_DIM, (g + 1) * POOL_GROUP_DIM)
        ug = u[:, cols]
        win = ug
        for k in range(1, w):
            win = win + ext_ref[POOL_HALO - k:POOL_HALO - k + tb, cols]
        count = jnp.minimum(t + 1, w).astype(F32)
        z = win / count - ug
        y = jnp.dot(z.astype(BF16), pw_ref[g], preferred_element_type=F32)
        yp_ref[:, cols] = (y * ps_ref[:, cols]).astype(yp_ref.dtype)
    ext_ref[0:POOL_HALO, :] = ext_ref[tb:tb + POOL_HALO, :]

    o1 = POOL_WIDTH
    o2 = o1 + Q_LORA_RANK
    o3 = o2 + KV_LORA_RANK
    cq = _rms(proj[:, o1:o2]) * qg_ref[...]
    ckv = _rms(proj[:, o2:o3]) * kvg_ref[...]
    qq = jnp.dot(cq.astype(BF16), wq_ref[...], preferred_element_type=F32)
    kv = jnp.dot(ckv.astype(BF16), wkv_ref[...], preferred_element_type=F32)
    ang = pos_ref[...] * invf_ref[...]
    cos = jnp.cos(ang)
    sin = jnp.sin(ang)
    sm_scale = QK_HEAD_DIM ** -0.5
    k_pe = proj[:, o3:o3 + LANES] * cos + proj[:, o3 + LANES:o3 + 2 * LANES] * sin
    k_pe = k_pe.astype(k_ref.dtype)
    rot_base = MLA_HEADS * QK_PAD_DIM
    for hd in range(MLA_HEADS):
        base = hd * QK_PAD_DIM
        q_ref[:, base:base + LANES] = (qq[:, base:base + LANES] * sm_scale).astype(q_ref.dtype)
        q_pe = (qq[:, base + LANES:base + 2 * LANES] * cos
                + qq[:, rot_base + hd * LANES:rot_base + (hd + 1) * LANES] * sin)
        q_ref[:, base + LANES:base + 2 * LANES] = (q_pe * sm_scale).astype(q_ref.dtype)
        k_ref[:, base:base + LANES] = kv[:, hd * LANES:(hd + 1) * LANES].astype(k_ref.dtype)
        k_ref[:, base + LANES:base + 2 * LANES] = k_pe
    v_ref[...] = kv[:, MLA_HEADS * QK_NOPE_DIM:].astype(v_ref.dtype)


def _mix_in(x, pos, shift, scale, g, w_in_p, pool_w, pool_scale, q_norm_g, wq_p, kv_norm_g, wkv_p,
            invf, tb=256):
    bsz, s, d = x.shape
    const2 = lambda b, i: (0, 0)
    tok = lambda b, i: (b, i, 0)
    per_b = lambda b, i: (b, 0, 0)
    n_qk = MLA_HEADS * QK_PAD_DIM
    n_v = MLA_HEADS * V_HEAD_DIM
    return pl.pallas_call(
        functools.partial(_mix_in_kernel, tb=tb),
        grid=(bsz, s // tb),
        in_specs=[pl.BlockSpec((None, tb, d), tok),
                  pl.BlockSpec((None, tb, 1), tok),
                  pl.BlockSpec((None, 1, d), per_b),
                  pl.BlockSpec((None, 1, d), per_b),
                  pl.BlockSpec((1, d), const2),
                  pl.BlockSpec(w_in_p.shape, const2),
                  pl.BlockSpec(pool_w.shape, lambda b, i: (0, 0, 0)),
                  pl.BlockSpec((1, POOL_WIDTH), const2),
                  pl.BlockSpec((1, Q_LORA_RANK), const2),
                  pl.BlockSpec(wq_p.shape, const2),
                  pl.BlockSpec((1, KV_LORA_RANK), const2),
                  pl.BlockSpec(wkv_p.shape, const2),
                  pl.BlockSpec((1, LANES), const2)],
        out_specs=[pl.BlockSpec((None, tb, POOL_WIDTH), tok),
                   pl.BlockSpec((None, tb, n_qk), tok),
                   pl.BlockSpec((None, tb, n_qk), tok),
                   pl.BlockSpec((None, tb, n_v), tok)],
        out_shape=[jax.ShapeDtypeStruct((bsz, s, POOL_WIDTH), BF16),
                   jax.ShapeDtypeStruct((bsz, s, n_qk), BF16),
                   jax.ShapeDtypeStruct((bsz, s, n_qk), BF16),
                   jax.ShapeDtypeStruct((bsz, s, n_v), BF16)],
        scratch_shapes=[pltpu.VMEM((tb + POOL_HALO, POOL_WIDTH), F32)],
        compiler_params=_params("parallel", "arbitrary"),
        name="mix_in",
    )(x, pos, shift, scale, g, w_in_p, pool_w, pool_scale, q_norm_g, wq_p, kv_norm_g, wkv_p, invf)


def _attn_kernel(q_ref, k_ref, v_ref, o_ref, *, tq):
    i = pl.program_id(2)
    q = q_ref[...]
    q_chunk = (i * tq + lax.broadcasted_iota(jnp.int32, (tq, tq), 0)) // CHUNK

    def step(kb, carry):
        m, l, acc = carry
        start = pl.multiple_of(kb * tq, tq)
        k = k_ref[pl.ds(start, tq), :]
        v = v_ref[pl.ds(start, tq), :]
        s = lax.dot_general(q, k, _NT, preferred_element_type=F32)
        k_chunk = (kb * tq + lax.broadcasted_iota(jnp.int32, (tq, tq), 1)) // CHUNK
        s = jnp.where(q_chunk >= k_chunk, s, NEG_INF)
        m_new = jnp.maximum(m, jnp.max(s, axis=-1, keepdims=True))
        p = jnp.exp(s - m_new)
        alpha = jnp.exp(m - m_new)
        l = alpha * l + jnp.sum(p, axis=-1, keepdims=True)
        acc = alpha * acc + jnp.dot(p.astype(v.dtype), v, preferred_element_type=F32)
        return m_new, l, acc

    init = (jnp.full((tq, 1), NEG_INF, F32), jnp.zeros((tq, 1), F32),
            jnp.zeros((tq, V_HEAD_DIM), F32))
    _, l, acc = lax.fori_loop(0, i + 1, step, init)
    o_ref[...] = (acc / l).astype(o_ref.dtype)


def _attn(q, k, v, tq=256):
    bsz, s, _ = q.shape
    return pl.pallas_call(
        functools.partial(_attn_kernel, tq=tq),
        grid=(bsz, MLA_HEADS, s // tq),
        in_specs=[pl.BlockSpec((None, tq, QK_PAD_DIM), lambda b, h, i: (b, i, h)),
                  pl.BlockSpec((None, s, QK_PAD_DIM), lambda b, h, i: (b, 0, h)),
                  pl.BlockSpec((None, s, V_HEAD_DIM), lambda b, h, i: (b, 0, h))],
        out_specs=pl.BlockSpec((None, tq, V_HEAD_DIM), lambda b, h, i: (b, i, h)),
        out_shape=jax.ShapeDtypeStruct((bsz, s, MLA_HEADS * V_HEAD_DIM), BF16),
        compiler_params=_params("parallel", "parallel", "arbitrary"),
        name="attn",
    )(q, k, v)


def _mix_out_kernel(yp_ref, ym_ref, x_ref, wo_ref, g1_ref, gate_ref, g2_ref, shift_ref, scale_ref,
                    x1_ref, h_ref):
    y = (jnp.dot(yp_ref[...], wo_ref[:POOL_WIDTH, :], preferred_element_type=F32)
         + jnp.dot(ym_ref[...], wo_ref[POOL_WIDTH:, :], preferred_element_type=F32))
    x1 = x_ref[...] + gate_ref[...] * (_rms(y) * g1_ref[...])
    x1_ref[...] = x1
    h = _rms(x1) * g2_ref[...]
    h_ref[...] = (h * (1.0 + scale_ref[...]) + shift_ref[...]).astype(h_ref.dtype)


def _mix_out(yp, ym, x, w_out, post_mix_g, gate1, pre_ffn_g, shift2, scale2, tb=256):
    bsz, s, d = x.shape
    const2 = lambda b, i: (0, 0)
    tok = lambda b, i: (b, i, 0)
    per_b = lambda b, i: (b, 0, 0)
    return pl.pallas_call(
        _mix_out_kernel,
        grid=(bsz, s // tb),
        in_specs=[pl.BlockSpec((None, tb, yp.shape[-1]), tok),
                  pl.BlockSpec((None, tb, ym.shape[-1]), tok),
                  pl.BlockSpec((None, tb, d), tok),
                  pl.BlockSpec(w_out.shape, const2),
                  pl.BlockSpec((1, d), const2),
                  pl.BlockSpec((None, 1, d), per_b),
                  pl.BlockSpec((1, d), const2),
                  pl.BlockSpec((None, 1, d), per_b),
                  pl.BlockSpec((None, 1, d), per_b)],
        out_specs=[pl.BlockSpec((None, tb, d), tok), pl.BlockSpec((None, tb, d), tok)],
        out_shape=[jax.ShapeDtypeStruct((bsz, s, d), F32), jax.ShapeDtypeStruct((bsz, s, d), BF16)],
        compiler_params=_params("parallel", "parallel"),
        name="mix_out",
    )(yp, ym, x, w_out, post_mix_g, gate1, pre_ffn_g, shift2, scale2)


def _peer_q_kernel(h_ref, wq_ref, keys_ref, s_ref):
    q = jnp.dot(h_ref[...], wq_ref[...], preferred_element_type=F32).astype(BF16)
    for g in range(2 * PEER_HEADS):
        qg = q[:, g * PEER_HALF_DIM:(g + 1) * PEER_HALF_DIM]
        s_ref[g] = lax.dot_general(keys_ref[g], qg, _NT, preferred_element_type=F32)


def _peer_q(h, wq, keys, tb=256):
    t, d = h.shape
    groups = 2 * PEER_HEADS
    return pl.pallas_call(
        _peer_q_kernel,
        grid=(t // tb,),
        in_specs=[pl.BlockSpec((tb, d), lambda i: (i, 0)),
                  pl.BlockSpec(wq.shape, lambda i: (0, 0)),
                  pl.BlockSpec(keys.shape, lambda i: (0, 0, 0))],
        out_specs=pl.BlockSpec((groups, PEER_N_KEYS, tb), lambda i: (0, 0, i)),
        out_shape=jax.ShapeDtypeStruct((groups, PEER_N_KEYS, t), F32),
        compiler_params=_params("parallel"),
        name="peer_q",
    )(h, wq, keys)


def _top_values(v, k):
    rows = lax.broadcasted_iota(jnp.int32, v.shape, 0)
    out = []
    for _ in range(k):
        m = jnp.max(v, axis=0, keepdims=True)
        out.append(m)
        first = jnp.min(jnp.where(v == m, rows, v.shape[0]), axis=0, keepdims=True)
        v = jnp.where(rows == first, -jnp.inf, v)
    return out


_CAND_PAIRS = [(k, l) for k in range(PEER_TOPK) for l in range(PEER_TOPK)
               if (k + 1) * (l + 1) <= PEER_TOPK]
_CAND_ROWS = -(-len(_CAND_PAIRS) // 8) * 8


def _route_kernel(s_ref, p_ref, tau_ref):
    tl = s_ref.shape[-1]
    for hd in range(PEER_HEADS):
        s1 = s_ref[2 * hd]
        s2 = s_ref[2 * hd + 1]
        a = _top_values(s1, PEER_TOPK)
        b = _top_values(s2, PEER_TOPK)
        cand = [a[k] + b[l] for k, l in _CAND_PAIRS]
        cand += [jnp.full((1, tl), -jnp.inf, F32)] * (_CAND_ROWS - len(cand))
        best = _top_values(jnp.concatenate(cand, axis=0), PEER_TOPK)
        z = jnp.zeros((1, tl), F32)
        for bv in best:
            z = z + jnp.exp(bv - best[0])
        tau_ref[hd:hd + 1, :] = best[-1]
        p_ref[2 * hd] = jnp.exp(s1 - a[0]) * (1.0 / z)
        p_ref[2 * hd + 1] = jnp.exp(s2 - b[0])


def _route(s_t, tl=256):
    groups, n_keys, t = s_t.shape
    return pl.pallas_call(
        _route_kernel,
        grid=(t // tl,),
        in_specs=[pl.BlockSpec((groups, n_keys, tl), lambda i: (0, 0, i))],
        out_specs=[pl.BlockSpec((groups, n_keys, tl), lambda i: (0, 0, i)),
                   pl.BlockSpec((PEER_HEADS, tl), lambda i: (0, i))],
        out_shape=[jax.ShapeDtypeStruct((groups, n_keys, t), F32),
                   jax.ShapeDtypeStruct((PEER_HEADS, t), F32)],
        compiler_params=_params("parallel"),
        name="route",
    )(s_t)


def _peer_kernel(h_ref, u_ref, vt_ref, s_ref, p_ref, srow_ref, prow_ref, tau_ref,
                 x1_ref, gate_ref, g_ref, o_ref, acc_ref, act_ref, *, keys_per_step, tc):
    j = pl.program_id(1)
    tb = h_ref.shape[0]

    @pl.when(j == 0)
    def _():
        acc_ref[...] = jnp.zeros_like(acc_ref)

    a_t = lax.dot_general(u_ref[...], h_ref[...], _NT, preferred_element_type=F32)

    sqrt_half = math.sqrt(0.5)
    for ii in range(keys_per_step):
        rows = slice(ii * PEER_N_KEYS, (ii + 1) * PEER_N_KEYS)
        for c in range(tb // tc):
            lanes = slice(c * tc, (c + 1) * tc)
            gate = jnp.zeros((PEER_N_KEYS, tc), F32)
            for hd in range(PEER_HEADS):
                s_sum = srow_ref[2 * hd, ii:ii + 1, lanes] + s_ref[2 * hd + 1, :, lanes]
                w = prow_ref[2 * hd, ii:ii + 1, lanes] * p_ref[2 * hd + 1, :, lanes]
                gate = gate + jnp.where(s_sum >= tau_ref[hd:hd + 1, lanes], w, 0.0)
            a = a_t[rows, lanes]
            act = 0.5 * a * (1.0 + lax.erf(a * sqrt_half)) * gate
            act_ref[rows, lanes] = act.astype(act_ref.dtype)

    acc_ref[...] += jnp.dot(vt_ref[...], act_ref[...], preferred_element_type=F32)

    @pl.when(j == pl.num_programs(1) - 1)
    def _():
        y_t = acc_ref[...]
        y_t = y_t * lax.rsqrt(jnp.mean(y_t * y_t, axis=0, keepdims=True) + EPS)
        o_ref[...] = x1_ref[...] + gate_ref[...] * (y_t.T * g_ref[...])


def _peer(h, u, vt, s_t, p_t, tau, x1, gate2, post_ffn_g, seq, tb=256, eb=1024, tc=128):
    t, d = h.shape
    n_exp = u.shape[0]
    groups = 2 * PEER_HEADS
    keys_per_step = eb // PEER_N_KEYS
    return pl.pallas_call(
        functools.partial(_peer_kernel, keys_per_step=keys_per_step, tc=tc),
        grid=(t // tb, n_exp // eb),
        in_specs=[pl.BlockSpec((tb, d), lambda i, j: (i, 0)),
                  pl.BlockSpec((eb, d), lambda i, j: (j, 0)),
                  pl.BlockSpec((d, eb), lambda i, j: (0, j)),
                  pl.BlockSpec((groups, PEER_N_KEYS, tb), lambda i, j: (0, 0, i)),
                  pl.BlockSpec((groups, PEER_N_KEYS, tb), lambda i, j: (0, 0, i)),
                  pl.BlockSpec((groups, keys_per_step, tb), lambda i, j: (0, j, i)),
                  pl.BlockSpec((groups, keys_per_step, tb), lambda i, j: (0, j, i)),
                  pl.BlockSpec((PEER_HEADS, tb), lambda i, j: (0, i)),
                  pl.BlockSpec((tb, d), lambda i, j: (i, 0)),
                  pl.BlockSpec((None, 1, d), lambda i, j: ((i * tb) // seq, 0, 0)),
                  pl.BlockSpec((1, d), lambda i, j: (0, 0))],
        out_specs=pl.BlockSpec((tb, d), lambda i, j: (i, 0)),
        out_shape=jax.ShapeDtypeStruct((t, d), F32),
        scratch_shapes=[pltpu.VMEM((d, tb), F32), pltpu.VMEM((eb, tb), BF16)],
        compiler_params=_params("parallel", "arbitrary"),
        name="peer",
    )(h, u, vt, s_t, p_t, s_t, p_t, tau, x1, gate2, post_ffn_g)


def _rot_half(w):
    half = w.shape[-1] // 2
    return jnp.concatenate([-w[..., half:], w[..., :half]], axis=-1)


def _prep_w_in(w_in):
    d = w_in.shape[0]
    o3 = POOL_WIDTH + Q_LORA_RANK + KV_LORA_RANK
    k_pe = w_in[:, o3:]
    pad = jnp.zeros((d, LANES - QK_ROPE_DIM), w_in.dtype)
    return jnp.concatenate([w_in[:, :o3], k_pe, pad, _rot_half(k_pe), pad], axis=1).astype(BF16)


def _prep_w_uq(w_uq):
    r = w_uq.shape[0]
    w3 = w_uq.reshape(r, MLA_HEADS, QK_HEAD_DIM)
    nope, pe = w3[..., :QK_NOPE_DIM], w3[..., QK_NOPE_DIM:]
    pad = jnp.zeros((r, MLA_HEADS, LANES - QK_ROPE_DIM), w_uq.dtype)
    plain = jnp.concatenate([nope, pe, pad], axis=-1).reshape(r, MLA_HEADS * QK_PAD_DIM)
    rot = jnp.concatenate([_rot_half(pe), pad], axis=-1).reshape(r, MLA_HEADS * LANES)
    return jnp.concatenate([plain, rot], axis=1).astype(BF16)


def _prep_w_ukv(w_ukv):
    r = w_ukv.shape[0]
    w3 = w_ukv.reshape(r, MLA_HEADS, QK_NOPE_DIM + V_HEAD_DIM)
    k_nope = w3[..., :QK_NOPE_DIM].reshape(r, MLA_HEADS * QK_NOPE_DIM)
    v = w3[..., QK_NOPE_DIM:].reshape(r, MLA_HEADS * V_HEAD_DIM)
    return jnp.concatenate([k_nope, v], axis=1).astype(BF16)


def kernel(x, c, positions, ada_w, ada_b, pre_mix_g, post_mix_g, pre_ffn_g, post_ffn_g, w_in, pool_w, pool_scale, q_norm_g, w_uq, kv_norm_g, w_ukv, w_out, peer_wq, peer_sub_keys, peer_u, peer_v):
    bsz, seq, d = x.shape
    depth = ada_w.shape[0]
    half = QK_ROPE_DIM // 2
    inv_freq = ROPE_THETA ** (-jnp.arange(half, dtype=F32) / half)
    invf = jnp.concatenate([inv_freq, inv_freq, jnp.zeros((LANES - QK_ROPE_DIM,), F32)]).reshape(1, LANES)
    pos = positions.astype(F32).reshape(bsz, seq, 1)
    for l in range(depth):
        mod = _ada(c, ada_w[l], ada_b[l])
        shift1, scale1, gate1, shift2, scale2, gate2 = [
            m.reshape(bsz, 1, d) for m in jnp.split(mod, 6, axis=-1)]
        yp, q, k, v = _mix_in(
            x, pos, shift1, scale1, pre_mix_g[l].reshape(1, d), _prep_w_in(w_in[l]),
            pool_w[l].astype(BF16), pool_scale[l].reshape(1, -1), q_norm_g[l].reshape(1, -1),
            _prep_w_uq(w_uq[l]), kv_norm_g[l].reshape(1, -1), _prep_w_ukv(w_ukv[l]), invf)
        ym = _attn(q, k, v)
        x1, h = _mix_out(yp, ym, x, w_out[l].astype(BF16), post_mix_g[l].reshape(1, d), gate1,
                         pre_ffn_g[l].reshape(1, d), shift2, scale2)
        h = h.reshape(bsz * seq, d)
        keys = peer_sub_keys[l].reshape(2 * PEER_HEADS, PEER_N_KEYS, PEER_HALF_DIM).astype(BF16)
        s_t = _peer_q(h, peer_wq[l].astype(BF16), keys)
        p_t, tau = _route(s_t)
        x = _peer(h, peer_u[l].astype(BF16), peer_v[l].T.astype(BF16), s_t, p_t, tau,
                  x1.reshape(bsz * seq, d), gate2, post_ffn_g[l].reshape(1, d), seq
                  ).reshape(bsz, seq, d)
    return x
```

```python
import functools
import math

import jax
import jax.numpy as jnp
from jax import lax
from jax.experimental import pallas as pl
from jax.experimental.pallas import tpu as pltpu

F32 = jnp.float32
BF16 = jnp.bfloat16

EPS = 1e-6
NEG_INF = -1e30
CHUNK = 64

POOL_WINDOWS = (2, 4, 8, 16)
POOL_GROUP_DIM = 256
POOL_WIDTH = 1024
POOL_HALO = 16

MLA_HEADS = 8
QK_NOPE_DIM = 128
QK_ROPE_DIM = 64
QK_HEAD_DIM = QK_NOPE_DIM + QK_ROPE_DIM
QK_PAD_DIM = 256
V_HEAD_DIM = 128
Q_LORA_RANK = 512
KV_LORA_RANK = 256
ROPE_THETA = 10000.0

PEER_HEADS = 8
PEER_N_KEYS = 128
PEER_HALF_DIM = 128
PEER_TOPK = 16
PEER_PIECE = 256

LANES = 128
VMEM_LIMIT_BYTES = 56 * 1024 * 1024

_NT = (((1,), (1,)), ((), ()))


def _rms(x):
    return x * lax.rsqrt(jnp.mean(x * x, axis=-1, keepdims=True) + EPS)


def _params(*semantics):
    return pltpu.CompilerParams(dimension_semantics=semantics, vmem_limit_bytes=VMEM_LIMIT_BYTES)


def _ada_kernel(c_ref, w_ref, b_ref, o_ref):
    c = c_ref[...]
    c_act = (c * jax.nn.sigmoid(c)).astype(BF16)
    o_ref[...] = jnp.dot(c_act, w_ref[...].astype(BF16), preferred_element_type=F32) + b_ref[...]


def _ada(c, w, b, block_n=1024):
    bsz, d = c.shape
    n = w.shape[1]
    return pl.pallas_call(
        _ada_kernel,
        grid=(n // block_n,),
        in_specs=[pl.BlockSpec((bsz, d), lambda j: (0, 0)),
                  pl.BlockSpec((d, block_n), lambda j: (0, j)),
                  pl.BlockSpec((1, block_n), lambda j: (0, j))],
        out_specs=pl.BlockSpec((bsz, block_n), lambda j: (0, j)),
        out_shape=jax.ShapeDtypeStruct((bsz, n), F32),
        compiler_params=_params("parallel"),
        name="ada",
    )(c, w, b.reshape(1, n))


def _mix_in_kernel(x_ref, pos_ref, shift_ref, scale_ref, g_ref, win_ref, pw_ref, ps_ref,
                   qg_ref, wq_ref, kvg_ref, wkv_ref, invf_ref,
                   yp_ref, q_ref, k_ref, v_ref, ext_ref, *, tb):
    i = pl.program_id(1)
    h = _rms(x_ref[...]) * g_ref[...]
    h = h * (1.0 + scale_ref[...]) + shift_ref[...]
    proj = jnp.dot(h.astype(BF16), win_ref[...], preferred_element_type=F32)

    u = proj[:, :POOL_WIDTH]

    @pl.when(i == 0)
    def _():
        ext_ref[0:POOL_HALO, :] = jnp.zeros((POOL_HALO, POOL_WIDTH), F32)

    ext_ref[POOL_HALO:, :] = u
    t = i * tb + lax.broadcasted_iota(jnp.int32, (tb, POOL_GROUP_DIM), 0)
    for g, w in enumerate(POOL_WINDOWS):
        cols = slice(g * POOL_GROUP_DIM, (g + 1) * POOL_GROUP_DIM)
        ug = u[:, cols]
        win = ug
        for k in range(1, w):
            win = win + ext_ref[POOL_HALO - k:POOL_HALO - k + tb, cols]
        count = jnp.minimum(t + 1, w).astype(F32)
        z = win / count - ug
        y = jnp.dot(z.astype(BF16), pw_ref[g], preferred_element_type=F32)
        yp_ref[:, cols] = (y * ps_ref[:, cols]).astype(yp_ref.dtype)
    ext_ref[0:POOL_HALO, :] = ext_ref[tb:tb + POOL_HALO, :]

    o1 = POOL_WIDTH
    o2 = o1 + Q_LORA_RANK
    o3 = o2 + KV_LORA_RANK
    cq = _rms(proj[:, o1:o2]) * qg_ref[...]
    ckv = _rms(proj[:, o2:o3]) * kvg_ref[...]
    qq = jnp.dot(cq.astype(BF16), wq_ref[...], preferred_element_type=F32)
    kv = jnp.dot(ckv.astype(BF16), wkv_ref[...], preferred_element_type=F32)
    ang = pos_ref[...] * invf_ref[...]
    cos = jnp.cos(ang)
    sin = jnp.sin(ang)
    sm_scale = QK_HEAD_DIM ** -0.5
    k_pe = proj[:, o3:o3 + LANES] * cos + proj[:, o3 + LANES:o3 + 2 * LANES] * sin
    k_pe = k_pe.astype(k_ref.dtype)
    rot_base = MLA_HEADS * QK_PAD_DIM
    for hd in range(MLA_HEADS):
        base = hd * QK_PAD_DIM
        q_ref[:, base:base + LANES] = (qq[:, base:base + LANES] * sm_scale).astype(q_ref.dtype)
        q_pe = (qq[:, base + LANES:base + 2 * LANES] * cos
                + qq[:, rot_base + hd * LANES:rot_base + (hd + 1) * LANES] * sin)
        q_ref[:, base + LANES:base + 2 * LANES] = (q_pe * sm_scale).astype(q_ref.dtype)
        k_ref[:, base:base + LANES] = kv[:, hd * LANES:(hd + 1) * LANES].astype(k_ref.dtype)
        k_ref[:, base + LANES:base + 2 * LANES] = k_pe
    v_ref[...] = kv[:, MLA_HEADS * QK_NOPE_DIM:].astype(v_ref.dtype)


def _mix_in(x, pos, shift, scale, g, w_in_p, pool_w, pool_scale, q_norm_g, wq_p, kv_norm_g, wkv_p,
            invf, tb=256):
    bsz, s, d = x.shape
    const2 = lambda b, i: (0, 0)
    tok = lambda b, i: (b, i, 0)
    per_b = lambda b, i: (b, 0, 0)
    n_qk = MLA_HEADS * QK_PAD_DIM
    n_v = MLA_HEADS * V_HEAD_DIM
    return pl.pallas_call(
        functools.partial(_mix_in_kernel, tb=tb),
        grid=(bsz, s // tb),
        in_specs=[pl.BlockSpec((None, tb, d), tok),
                  pl.BlockSpec((None, tb, 1), tok),
                  pl.BlockSpec((None, 1, d), per_b),
                  pl.BlockSpec((None, 1, d), per_b),
                  pl.BlockSpec((1, d), const2),
                  pl.BlockSpec(w_in_p.shape, const2),
                  pl.BlockSpec(pool_w.shape, lambda b, i: (0, 0, 0)),
                  pl.BlockSpec((1, POOL_WIDTH), const2),
                  pl.BlockSpec((1, Q_LORA_RANK), const2),
                  pl.BlockSpec(wq_p.shape, const2),
                  pl.BlockSpec((1, KV_LORA_RANK), const2),
                  pl.BlockSpec(wkv_p.shape, const2),
                  pl.BlockSpec((1, LANES), const2)],
        out_specs=[pl.BlockSpec((None, tb, POOL_WIDTH), tok),
                   pl.BlockSpec((None, tb, n_qk), tok),
                   pl.BlockSpec((None, tb, n_qk), tok),
                   pl.BlockSpec((None, tb, n_v), tok)],
        out_shape=[jax.ShapeDtypeStruct((bsz, s, POOL_WIDTH), BF16),
                   jax.ShapeDtypeStruct((bsz, s, n_qk), BF16),
                   jax.ShapeDtypeStruct((bsz, s, n_qk), BF16),
                   jax.ShapeDtypeStruct((bsz, s, n_v), BF16)],
        scratch_shapes=[pltpu.VMEM((tb + POOL_HALO, POOL_WIDTH), F32)],
        compiler_params=_params("parallel", "arbitrary"),
        name="mix_in",
    )(x, pos, shift, scale, g, w_in_p, pool_w, pool_scale, q_norm_g, wq_p, kv_norm_g, wkv_p, invf)


def _attn_kernel(q_ref, k_ref, v_ref, o_ref, *, tq):
    i = pl.program_id(2)
    q = q_ref[...]
    q_chunk = (i * tq + lax.broadcasted_iota(jnp.int32, (tq, tq), 0)) // CHUNK

    def step(kb, carry):
        m, l, acc = carry
        start = pl.multiple_of(kb * tq, tq)
        k = k_ref[pl.ds(start, tq), :]
        v = v_ref[pl.ds(start, tq), :]
        s = lax.dot_general(q, k, _NT, preferred_element_type=F32)
        k_chunk = (kb * tq + lax.broadcasted_iota(jnp.int32, (tq, tq), 1)) // CHUNK
        s = jnp.where(q_chunk >= k_chunk, s, NEG_INF)
        m_new = jnp.maximum(m, jnp.max(s, axis=-1, keepdims=True))
        p = jnp.exp(s - m_new)
        alpha = jnp.exp(m - m_new)
        l = alpha * l + jnp.sum(p, axis=-1, keepdims=True)
        acc = alpha * acc + jnp.dot(p.astype(v.dtype), v, preferred_element_type=F32)
        return m_new, l, acc

    init = (jnp.full((tq, 1), NEG_INF, F32), jnp.zeros((tq, 1), F32),
            jnp.zeros((tq, V_HEAD_DIM), F32))
    _, l, acc = lax.fori_loop(0, i + 1, step, init)
    o_ref[...] = (acc / l).astype(o_ref.dtype)


def _attn(q, k, v, tq=256):
    bsz, s, _ = q.shape
    return pl.pallas_call(
        functools.partial(_attn_kernel, tq=tq),
        grid=(bsz, MLA_HEADS, s // tq),
        in_specs=[pl.BlockSpec((None, tq, QK_PAD_DIM), lambda b, h, i: (b, i, h)),
                  pl.BlockSpec((None, s, QK_PAD_DIM), lambda b, h, i: (b, 0, h)),
                  pl.BlockSpec((None, s, V_HEAD_DIM), lambda b, h, i: (b, 0, h))],
        out_specs=pl.BlockSpec((None, tq, V_HEAD_DIM), lambda b, h, i: (b, i, h)),
        out_shape=jax.ShapeDtypeStruct((bsz, s, MLA_HEADS * V_HEAD_DIM), BF16),
        compiler_params=_params("parallel", "parallel", "arbitrary"),
        name="attn",
    )(q, k, v)


def _mix_out_kernel(yp_ref, ym_ref, x_ref, wo_ref, g1_ref, gate_ref, g2_ref, shift_ref, scale_ref,
                    x1_ref, h_ref):
    y = (jnp.dot(yp_ref[...], wo_ref[:POOL_WIDTH, :], preferred_element_type=F32)
         + jnp.dot(ym_ref[...], wo_ref[POOL_WIDTH:, :], preferred_element_type=F32))
    x1 = x_ref[...] + gate_ref[...] * (_rms(y) * g1_ref[...])
    x1_ref[...] = x1
    h = _rms(x1) * g2_ref[...]
    h_ref[...] = (h * (1.0 + scale_ref[...]) + shift_ref[...]).astype(h_ref.dtype)


def _mix_out(yp, ym, x, w_out, post_mix_g, gate1, pre_ffn_g, shift2, scale2, tb=256):
    bsz, s, d = x.shape
    const2 = lambda b, i: (0, 0)
    tok = lambda b, i: (b, i, 0)
    per_b = lambda b, i: (b, 0, 0)
    return pl.pallas_call(
        _mix_out_kernel,
        grid=(bsz, s // tb),
        in_specs=[pl.BlockSpec((None, tb, yp.shape[-1]), tok),
                  pl.BlockSpec((None, tb, ym.shape[-1]), tok),
                  pl.BlockSpec((None, tb, d), tok),
                  pl.BlockSpec(w_out.shape, const2),
                  pl.BlockSpec((1, d), const2),
                  pl.BlockSpec((None, 1, d), per_b),
                  pl.BlockSpec((1, d), const2),
                  pl.BlockSpec((None, 1, d), per_b),
                  pl.BlockSpec((None, 1, d), per_b)],
        out_specs=[pl.BlockSpec((None, tb, d), tok), pl.BlockSpec((None, tb, d), tok)],
        out_shape=[jax.ShapeDtypeStruct((bsz, s, d), F32), jax.ShapeDtypeStruct((bsz, s, d), BF16)],
        compiler_params=_params("parallel", "parallel"),
        name="mix_out",
    )(yp, ym, x, w_out, post_mix_g, gate1, pre_ffn_g, shift2, scale2)


def _peer_q_kernel(h_ref, wq_ref, keys_ref, s_ref):
    q = jnp.dot(h_ref[...], wq_ref[...], preferred_element_type=F32).astype(BF16)
    for g in range(2 * PEER_HEADS):
        qg = q[:, g * PEER_HALF_DIM:(g + 1) * PEER_HALF_DIM]
        s_ref[g] = lax.dot_general(keys_ref[g], qg, _NT, preferred_element_type=F32)


def _peer_q(h, wq, keys, tb=256):
    t, d = h.shape
    groups = 2 * PEER_HEADS
    return pl.pallas_call(
        _peer_q_kernel,
        grid=(t // tb,),
        in_specs=[pl.BlockSpec((tb, d), lambda i: (i, 0)),
                  pl.BlockSpec(wq.shape, lambda i: (0, 0)),
                  pl.BlockSpec(keys.shape, lambda i: (0, 0, 0))],
        out_specs=pl.BlockSpec((groups, PEER_N_KEYS, tb), lambda i: (0, 0, i)),
        out_shape=jax.ShapeDtypeStruct((groups, PEER_N_KEYS, t), F32),
        compiler_params=_params("parallel"),
        name="peer_q",
    )(h, wq, keys)


def _top_values(v, k):
    rows = lax.broadcasted_iota(jnp.int32, v.shape, 0)
    out = []
    for _ in range(k):
        m = jnp.max(v, axis=0, keepdims=True)
        out.append(m)
        first = jnp.min(jnp.where(v == m, rows, v.shape[0]), axis=0, keepdims=True)
        v = jnp.where(rows == first, -jnp.inf, v)
    return out


_CAND_PAIRS = [(k, l) for k in range(PEER_TOPK) for l in range(PEER_TOPK)
               if (k + 1) * (l + 1) <= PEER_TOPK]
_CAND_ROWS = -(-len(_CAND_PAIRS) // 8) * 8


def _dup_bf16(x):
    hi = pltpu.bitcast(x.astype(BF16).astype(F32), jnp.uint32)
    return hi | (hi >> 16)


def _route_kernel(s_ref, cnt_ref, p1_ref, rank_ref, p2_ref):
    tl = s_ref.shape[-1]
    for hd in range(PEER_HEADS):
        s1 = s_ref[2 * hd]
        s2 = s_ref[2 * hd + 1]
        a = _top_values(s1, PEER_TOPK)
        b = _top_values(s2, PEER_TOPK)
        cand = [a[k] + b[l] for k, l in _CAND_PAIRS]
        cand += [jnp.full((1, tl), -jnp.inf, F32)] * (_CAND_ROWS - len(cand))
        best = _top_values(jnp.concatenate(cand, axis=0), PEER_TOPK)
        tau = best[-1]
        z = jnp.zeros((1, tl), F32)
        for bv in best:
            z = z + jnp.exp(bv - best[0])
        cnt = jnp.zeros_like(s1)
        rank = jnp.zeros_like(s2)
        for l in range(PEER_TOPK):
            cnt = cnt + jnp.where(s1 + b[l] >= tau, 1.0, 0.0)
            rank = rank + jnp.where(b[l] > s2, 1.0, 0.0)
        cnt_ref[hd] = _dup_bf16(cnt)
        p1_ref[hd] = _dup_bf16(jnp.exp(s1 - a[0]) * (0.5 / z))
        rank_ref[hd] = rank.astype(rank_ref.dtype)
        p2_ref[hd] = jnp.exp(s2 - b[0]).astype(p2_ref.dtype)


def _route(s_t, tl=256):
    groups, n_keys, t = s_t.shape
    out_spec = pl.BlockSpec((PEER_HEADS, n_keys, tl), lambda i: (0, 0, i))
    return pl.pallas_call(
        _route_kernel,
        grid=(t // tl,),
        in_specs=[pl.BlockSpec((groups, n_keys, tl), lambda i: (0, 0, i))],
        out_specs=[out_spec] * 4,
        out_shape=[jax.ShapeDtypeStruct((PEER_HEADS, n_keys, t), jnp.uint32),
                   jax.ShapeDtypeStruct((PEER_HEADS, n_keys, t), jnp.uint32),
                   jax.ShapeDtypeStruct((PEER_HEADS, n_keys, t), BF16),
                   jax.ShapeDtypeStruct((PEER_HEADS, n_keys, t), BF16)],
        compiler_params=_params("parallel"),
        name="route",
    )(s_t)


def _peer_kernel(h_ref, u_ref, vt_ref, rank_ref, p2_ref, cnt_ref, p1_ref, x1_ref, gate_ref, g_ref,
                 o_ref, acc_ref, a_ref, gt_ref, act_ref, *, pieces, tc):
    j = pl.program_id(1)
    n_blocks = pl.num_programs(1) - 1
    tb = h_ref.shape[0]
    keys_per_piece = PEER_PIECE // PEER_N_KEYS
    sqrt_half = math.sqrt(0.5)
    bf16_rows = 16
    groups = PEER_N_KEYS // bf16_rows

    def scores_and_gate(s, slot):
        rows = slice(s * PEER_PIECE, (s + 1) * PEER_PIECE)
        a_ref[slot] = lax.dot_general(u_ref[rows, :], h_ref[...], _NT, preferred_element_type=F32)
        for i2 in range(keys_per_piece):
            ii = s * keys_per_piece + i2
            for c in range(tb // tc):
                lanes = slice(c * tc, (c + 1) * tc)
                gates = [None] * groups
                for hd in range(PEER_HEADS):
                    cnt = pltpu.bitcast(jnp.broadcast_to(cnt_ref[hd, ii:ii + 1, lanes], (8, tc)), BF16)
                    p1 = pltpu.bitcast(jnp.broadcast_to(p1_ref[hd, ii:ii + 1, lanes], (8, tc)), BF16)
                    for r in range(groups):
                        rr = slice(r * bf16_rows, (r + 1) * bf16_rows)
                        sel = jnp.clip(cnt - rank_ref[hd, rr, lanes], 0.0, 1.0)
                        g = (p1 * p2_ref[hd, rr, lanes]) * sel
                        gates[r] = g if hd == 0 else gates[r] + g
                for r in range(groups):
                    lo = i2 * PEER_N_KEYS + r * bf16_rows
                    gt_ref[slot, lo:lo + bf16_rows, lanes] = gates[r]

    def combine(s, slot):
        for c in range(tb // tc):
            lanes = slice(c * tc, (c + 1) * tc)
            a = a_ref[slot, :, lanes]
            e = a * (1.0 + lax.erf(a * sqrt_half))
            act_ref[:, lanes] = e.astype(BF16) * gt_ref[slot, :, lanes]
        cols = slice(s * PEER_PIECE, (s + 1) * PEER_PIECE)
        acc_ref[...] += jnp.dot(vt_ref[:, cols], act_ref[...], preferred_element_type=F32)

    @pl.when(j == 0)
    def _():
        acc_ref[...] = jnp.zeros_like(acc_ref)
        a_ref[1] = jnp.zeros(a_ref.shape[1:], a_ref.dtype)
        gt_ref[1] = jnp.zeros(gt_ref.shape[1:], gt_ref.dtype)

    @pl.when(j < n_blocks)
    def _():
        for s in range(pieces):
            scores_and_gate(s, s % 2)
            combine(s, (s + 1) % 2)

    @pl.when(j == n_blocks)
    def _():
        combine(0, (pieces - 1) % 2)
        y_t = acc_ref[...]
        y_t = y_t * lax.rsqrt(jnp.mean(y_t * y_t, axis=0, keepdims=True) + EPS)
        o_ref[...] = x1_ref[...] + gate_ref[...] * (y_t.T * g_ref[...])


def _peer(h, u, vt_shifted, cnt1, p1, rank2, p2, x1, gate2, post_ffn_g, seq, tb=512, eb=1024, tc=128):
    t, d = h.shape
    n_blocks = u.shape[0] // eb
    pieces = eb // PEER_PIECE
    assert pieces % 2 == 0 and vt_shifted.shape[1] == (n_blocks + 1) * eb
    keys_per_step = eb // PEER_N_KEYS
    cur = lambda j: jnp.minimum(j, n_blocks - 1)
    tok_tile = pl.BlockSpec((PEER_HEADS, PEER_N_KEYS, tb), lambda i, j: (0, 0, i))
    key_rows = pl.BlockSpec((PEER_HEADS, keys_per_step, tb), lambda i, j: (0, cur(j), i))
    return pl.pallas_call(
        functools.partial(_peer_kernel, pieces=pieces, tc=tc),
        grid=(t // tb, n_blocks + 1),
        in_specs=[pl.BlockSpec((tb, d), lambda i, j: (i, 0)),
                  pl.BlockSpec((eb, d), lambda i, j: (cur(j), 0)),
                  pl.BlockSpec((d, eb), lambda i, j: (0, j)),
                  tok_tile, tok_tile, key_rows, key_rows,
                  pl.BlockSpec((tb, d), lambda i, j: (i, 0)),
                  pl.BlockSpec((None, 1, d), lambda i, j: ((i * tb) // seq, 0, 0)),
                  pl.BlockSpec((1, d), lambda i, j: (0, 0))],
        out_specs=pl.BlockSpec((tb, d), lambda i, j: (i, 0)),
        out_shape=jax.ShapeDtypeStruct((t, d), F32),
        scratch_shapes=[pltpu.VMEM((d, tb), F32),
                        pltpu.VMEM((2, PEER_PIECE, tb), F32),
                        pltpu.VMEM((2, PEER_PIECE, tb), BF16),
                        pltpu.VMEM((PEER_PIECE, tb), BF16)],
        compiler_params=_params("parallel", "arbitrary"),
        name="peer",
    )(h, u, vt_shifted, rank2, p2, cnt1, p1, x1, gate2, post_ffn_g)


def _rot_half(w):
    half = w.shape[-1] // 2
    return jnp.concatenate([-w[..., half:], w[..., :half]], axis=-1)


def _prep_w_in(w_in):
    d = w_in.shape[0]
    o3 = POOL_WIDTH + Q_LORA_RANK + KV_LORA_RANK
    k_pe = w_in[:, o3:]
    pad = jnp.zeros((d, LANES - QK_ROPE_DIM), w_in.dtype)
    return jnp.concatenate([w_in[:, :o3], k_pe, pad, _rot_half(k_pe), pad], axis=1).astype(BF16)


def _prep_w_uq(w_uq):
    r = w_uq.shape[0]
    w3 = w_uq.reshape(r, MLA_HEADS, QK_HEAD_DIM)
    nope, pe = w3[..., :QK_NOPE_DIM], w3[..., QK_NOPE_DIM:]
    pad = jnp.zeros((r, MLA_HEADS, LANES - QK_ROPE_DIM), w_uq.dtype)
    plain = jnp.concatenate([nope, pe, pad], axis=-1).reshape(r, MLA_HEADS * QK_PAD_DIM)
    rot = jnp.concatenate([_rot_half(pe), pad], axis=-1).reshape(r, MLA_HEADS * LANES)
    return jnp.concatenate([plain, rot], axis=1).astype(BF16)


def _prep_w_ukv(w_ukv):
    r = w_ukv.shape[0]
    w3 = w_ukv.reshape(r, MLA_HEADS, QK_NOPE_DIM + V_HEAD_DIM)
    k_nope = w3[..., :QK_NOPE_DIM].reshape(r, MLA_HEADS * QK_NOPE_DIM)
    v = w3[..., QK_NOPE_DIM:].reshape(r, MLA_HEADS * V_HEAD_DIM)
    return jnp.concatenate([k_nope, v], axis=1).astype(BF16)


def kernel(x, c, positions, ada_w, ada_b, pre_mix_g, post_mix_g, pre_ffn_g, post_ffn_g, w_in, pool_w, pool_scale, q_norm_g, w_uq, kv_norm_g, w_ukv, w_out, peer_wq, peer_sub_keys, peer_u, peer_v):
    bsz, seq, d = x.shape
    depth = ada_w.shape[0]
    half = QK_ROPE_DIM // 2
    inv_freq = ROPE_THETA ** (-jnp.arange(half, dtype=F32) / half)
    invf = jnp.concatenate([inv_freq, inv_freq, jnp.zeros((LANES - QK_ROPE_DIM,), F32)]).reshape(1, LANES)
    pos = positions.astype(F32).reshape(bsz, seq, 1)
    for l in range(depth):
        mod = _ada(c, ada_w[l], ada_b[l])
        shift1, scale1, gate1, shift2, scale2, gate2 = [
            m.reshape(bsz, 1, d) for m in jnp.split(mod, 6, axis=-1)]
        yp, q, k, v = _mix_in(
            x, pos, shift1, scale1, pre_mix_g[l].reshape(1, d), _prep_w_in(w_in[l]),
            pool_w[l].astype(BF16), pool_scale[l].reshape(1, -1), q_norm_g[l].reshape(1, -1),
            _prep_w_uq(w_uq[l]), kv_norm_g[l].reshape(1, -1), _prep_w_ukv(w_ukv[l]), invf)
        ym = _attn(q, k, v)
        x1, h = _mix_out(yp, ym, x, w_out[l].astype(BF16), post_mix_g[l].reshape(1, d), gate1,
                         pre_ffn_g[l].reshape(1, d), shift2, scale2)
        h = h.reshape(bsz * seq, d)
        keys = peer_sub_keys[l].reshape(2 * PEER_HEADS, PEER_N_KEYS, PEER_HALF_DIM).astype(BF16)
        s_t = _peer_q(h, peer_wq[l].astype(BF16), keys)
        cnt1, p1, rank2, p2 = _route(s_t)
        eb = 1024
        vt_shifted = jnp.pad(peer_v[l].T.astype(BF16), ((0, 0), (PEER_PIECE, eb - PEER_PIECE)))
        x = _peer(h, peer_u[l].astype(BF16), vt_shifted, cnt1, p1, rank2, p2,
                  x1.reshape(bsz * seq, d), gate2, post_ffn_g[l].reshape(1, d), seq
                  ).reshape(bsz, seq, d)
    return x
```
